```python
import math
import jax, jax.numpy as jnp
from jax import lax
import numpy as np

D_MODEL = 1024
BATCH = 8
SEQ = 4096
DEPTH = 4
DEC_BATCH = 8
DEC_SEQ = 8192
PAST_LEN = 128

HEAD_DIM = 64
N_Q_HEADS = 16
N_KV_HEADS = 4
GROUP = N_Q_HEADS // N_KV_HEADS
Q_DIM = N_Q_HEADS * HEAD_DIM
KV_DIM = N_KV_HEADS * HEAD_DIM
Q_BLOCK = 128
ROPE_THETA = 10000.0
AXIS_DIM = HEAD_DIM // 2
GRID_W = 64
CONV_DIM = D_MODEL
CONV_K = 31
CONV_PAD = (CONV_K - 1) // 2
GATE_DIM = 2 * D_MODEL
IN_DIM = Q_DIM + 2 * KV_DIM + 2 * CONV_DIM + GATE_DIM
D_FF = 4 * D_MODEL
EPS = 1e-6

kernel_name = "gated_parallel_gqa_conformer_encoder"


def rmsnorm(x, g):
    xf = x.astype(jnp.float32)
    y = xf * lax.rsqrt(jnp.mean(jnp.square(xf), axis=-1, keepdims=True) + EPS)
    return (y * g.astype(jnp.float32)).astype(x.dtype)


def layernorm(x, g, b):
    xf = x.astype(jnp.float32)
    mu = jnp.mean(xf, axis=-1, keepdims=True)
    var = jnp.mean(jnp.square(xf - mu), axis=-1, keepdims=True)
    y = (xf - mu) * lax.rsqrt(var + EPS)
    return (y * g.astype(jnp.float32) + b.astype(jnp.float32)).astype(x.dtype)


def axial_rope_tables(T):
    rows = T // GRID_W
    row = jnp.repeat(jnp.arange(rows, dtype=jnp.float32), GRID_W)
    col = jnp.tile(jnp.arange(GRID_W, dtype=jnp.float32), rows)
    freqs = ROPE_THETA ** (-jnp.arange(0, AXIS_DIM, 2, dtype=jnp.float32) / AXIS_DIM)
    ang_r = row[:, None] * freqs[None, :]
    ang_c = col[:, None] * freqs[None, :]
    return jnp.cos(ang_r), jnp.sin(ang_r), jnp.cos(ang_c), jnp.sin(ang_c)


def rope_half(x, cos, sin):
    c = cos[None, :, None, :]
    s = sin[None, :, None, :]
    x1, x2 = jnp.split(x, 2, axis=-1)
    return jnp.concatenate([x1 * c - x2 * s, x2 * c + x1 * s], axis=-1)


def apply_axial_rope(x, tables):
    cos_r, sin_r, cos_c, sin_c = tables
    xf = x.astype(jnp.float32)
    y = jnp.concatenate([rope_half(xf[..., :AXIS_DIM], cos_r, sin_r),
                         rope_half(xf[..., AXIS_DIM:], cos_c, sin_c)], axis=-1)
    return y.astype(x.dtype)


def blocked_gqa(q, k, v):
    B, T = q.shape[0], q.shape[1]
    nb = T // Q_BLOCK
    scale = 1.0 / math.sqrt(HEAD_DIM)
    qg = q.reshape(B, nb, Q_BLOCK, N_KV_HEADS, GROUP, HEAD_DIM).transpose(1, 0, 2, 3, 4, 5)

    def block(qb):
        s = jnp.einsum('bqkgd,bskd->bkgqs', qb, k, preferred_element_type=jnp.float32) * scale
        p = jax.nn.softmax(s, axis=-1).astype(v.dtype)
        return jnp.einsum('bkgqs,bskd->bqkgd', p, v)

    o = lax.map(block, qg)
    return o.transpose(1, 0, 2, 3, 4, 5).reshape(B, T, Q_DIM)


def depthwise_conv(u, w, b):
    y = lax.conv_general_dilated(u, w[:, None, :], window_strides=(1,),
                                 padding=[(CONV_PAD, CONV_PAD)],
                                 dimension_numbers=('NWC', 'WIO', 'NWC'),
                                 feature_group_count=CONV_DIM)
    return y + b


def trunk(x, g_mix, w_in, q_norm, k_norm, conv_dw, conv_dw_b, conv_ln_g, conv_ln_b,
          w_conv_out, w_o, g_mlp, w_up, w_down, g_final):
    B, T, _ = x.shape
    tables = axial_rope_tables(T)
    splits = [Q_DIM, Q_DIM + KV_DIM, Q_DIM + 2 * KV_DIM, Q_DIM + 2 * KV_DIM + 2 * CONV_DIM]
    for l in range(DEPTH):
        h = rmsnorm(x, g_mix[l])
        proj = h @ w_in[l]
        q, k, v, u, gates = jnp.split(proj, splits, axis=-1)
        q = rmsnorm(q.reshape(B, T, N_Q_HEADS, HEAD_DIM), q_norm[l])
        k = rmsnorm(k.reshape(B, T, N_KV_HEADS, HEAD_DIM), k_norm[l])
        v = v.reshape(B, T, N_KV_HEADS, HEAD_DIM)
        q = apply_axial_rope(q, tables)
        k = apply_axial_rope(k, tables)
        attn = blocked_gqa(q, k, v)
        ua, ub = jnp.split(u, 2, axis=-1)
        c = ua * jax.nn.sigmoid(ub)
        c = depthwise_conv(c, conv_dw[l], conv_dw_b[l])
        c = jax.nn.silu(layernorm(c, conv_ln_g[l], conv_ln_b[l]))
        conv = c @ w_conv_out[l]
        ga, gc = jnp.split(gates, 2, axis=-1)
        merged = jax.nn.sigmoid(ga) * attn + jax.nn.sigmoid(gc) * conv
        x = x + merged @ w_o[l]
        h = rmsnorm(x, g_mlp[l])
        x = x + jnp.square(jax.nn.relu(h @ w_up[l])) @ w_down[l]
    return rmsnorm(x, g_final)


def setup_inputs(seed: int = 0) -> dict:
    key = jax.random.key(seed)
    ks = jax.random.split(key, 16)
    f32 = jnp.float32

    def nrm(k, shape, scale):
        return jax.random.normal(k, shape, f32) * scale

    return {
        "x_prompt": nrm(ks[0], (BATCH, SEQ, D_MODEL), 1.0),
        "x_sample": nrm(ks[1], (DEC_BATCH, DEC_SEQ, D_MODEL), 1.0),
        "g_mix": 1.0 + nrm(ks[2], (DEPTH, D_MODEL), 0.02),
        "w_in": nrm(ks[3], (DEPTH, D_MODEL, IN_DIM), D_MODEL ** -0.5),
        "q_norm": 1.0 + nrm(ks[4], (DEPTH, HEAD_DIM), 0.02),
        "k_norm": 1.0 + nrm(ks[5], (DEPTH, HEAD_DIM), 0.02),
        "conv_dw": nrm(ks[6], (DEPTH, CONV_K, CONV_DIM), CONV_K ** -0.5),
        "conv_dw_b": nrm(ks[7], (DEPTH, CONV_DIM), 0.01),
        "conv_ln_g": 1.0 + nrm(ks[8], (DEPTH, CONV_DIM), 0.02),
        "conv_ln_b": nrm(ks[9], (DEPTH, CONV_DIM), 0.01),
        "w_conv_out": nrm(ks[10], (DEPTH, CONV_DIM, D_MODEL), CONV_DIM ** -0.5),
        "w_o": nrm(ks[11], (DEPTH, D_MODEL, D_MODEL), D_MODEL ** -0.5),
        "g_mlp": 1.0 + nrm(ks[12], (DEPTH, D_MODEL), 0.02),
        "w_up": nrm(ks[13], (DEPTH, D_MODEL, D_FF), D_MODEL ** -0.5),
        "w_down": nrm(ks[14], (DEPTH, D_FF, D_MODEL), D_FF ** -0.5),
        "g_final": 1.0 + nrm(ks[15], (D_MODEL,), 0.02),
    }


def reference(x_prompt, x_sample, g_mix, w_in, q_norm, k_norm, conv_dw, conv_dw_b,
              conv_ln_g, conv_ln_b, w_conv_out, w_o, g_mlp, w_up, w_down, g_final):
    y_prompt = trunk(x_prompt, g_mix, w_in, q_norm, k_norm, conv_dw, conv_dw_b, conv_ln_g,
                     conv_ln_b, w_conv_out, w_o, g_mlp, w_up, w_down, g_final)
    y_sample = trunk(x_sample, g_mix, w_in, q_norm, k_norm, conv_dw, conv_dw_b, conv_ln_g,
                     conv_ln_b, w_conv_out, w_o, g_mlp, w_up, w_down, g_final)
    return (y_prompt, y_sample)
```

```python
import functools
import math

import jax
import jax.numpy as jnp
from jax import lax
from jax.experimental import pallas as pl
from jax.experimental.pallas import tpu as pltpu

D_MODEL = 1024
DEPTH = 4
HEAD_DIM = 64
N_Q_HEADS = 16
N_KV_HEADS = 4
GROUP = N_Q_HEADS // N_KV_HEADS
Q_DIM = N_Q_HEADS * HEAD_DIM
KV_DIM = N_KV_HEADS * HEAD_DIM
QKV_DIM = Q_DIM + 2 * KV_DIM
ROPE_THETA = 10000.0
AXIS_DIM = HEAD_DIM // 2
HALF_AXIS = AXIS_DIM // 2
GRID_W = 64
CONV_DIM = D_MODEL
CONV_K = 31
CONV_PAD = (CONV_K - 1) // 2
D_FF = 4 * D_MODEL
EPS = 1e-6

SUBLANES = 8
HALO = 16
VMEM_LIMIT = 56 * 1024 * 1024

Q_SCALE = math.log2(math.e) / math.sqrt(HEAD_DIM)

BF16 = jnp.bfloat16
F32 = jnp.float32

PROJ_TM = 512
ATTN_TQ = 512
MIX_TM = 256
MIX_ROWS = 32
MLP_TM = 512
MLP_FC = 1024


def _dot(a, b):
    return jnp.dot(a, b, preferred_element_type=F32)


def _dot_nt(a, b):
    return lax.dot_general(a, b, (((1,), (1,)), ((), ())), preferred_element_type=F32)


def _sigmoid(x):
    return 1.0 / (1.0 + jnp.exp(-x))


def _rmsnorm_rows(x, g):
    ms = jnp.mean(x * x, axis=-1, keepdims=True)
    return x * lax.rsqrt(ms + EPS) * g


def _const_spec(shape, index_map):
    return pl.BlockSpec(shape, index_map, pipeline_mode=pl.Buffered(1))


def _proj_kernel(x_ref, gmix_ref, wqkv_ref, wu_ref, qg_ref, kg_ref, rope_ref,
                 qt_ref, k_ref, vt_ref, c_ref):
    x = x_ref[0]
    h = _rmsnorm_rows(x, gmix_ref[0]).astype(BF16)

    qkv_t = _dot_nt(wqkv_ref[0], h)
    rope = rope_ref[...]
    cr, sr = rope[0:HALF_AXIS], rope[HALF_AXIS:2 * HALF_AXIS]
    cc, sc = rope[2 * HALF_AXIS:3 * HALF_AXIS], rope[3 * HALF_AXIS:4 * HALF_AXIS]

    def norm_rope(a, g):
        r = lax.rsqrt(jnp.mean(a * a, axis=0, keepdims=True) + EPS)
        an = a * r * g
        x1r, x2r = an[0:HALF_AXIS], an[HALF_AXIS:2 * HALF_AXIS]
        x1c, x2c = an[2 * HALF_AXIS:3 * HALF_AXIS], an[3 * HALF_AXIS:4 * HALF_AXIS]
        return jnp.concatenate(
            [x1r * cr - x2r * sr, x2r * cr + x1r * sr,
             x1c * cc - x2c * sc, x2c * cc + x1c * sc], axis=0)

    qg = qg_ref[0] * Q_SCALE
    kg = kg_ref[0]
    for hh in range(N_Q_HEADS):
        a = qkv_t[hh * HEAD_DIM:(hh + 1) * HEAD_DIM]
        qt_ref[0, hh * HEAD_DIM:(hh + 1) * HEAD_DIM, :] = norm_rope(a, qg).astype(BF16)
    for hh in range(N_KV_HEADS):
        a = qkv_t[Q_DIM + hh * HEAD_DIM:Q_DIM + (hh + 1) * HEAD_DIM]
        k_ref[0, hh] = norm_rope(a, kg).T.astype(BF16)
    vt_ref[0, 0] = qkv_t[Q_DIM + KV_DIM:].astype(BF16)

    half = CONV_DIM // 2
    for j in range(2):
        ua = _dot(h, wu_ref[0, :, j * half:(j + 1) * half])
        ub = _dot(h, wu_ref[0, :, CONV_DIM + j * half:CONV_DIM + (j + 1) * half])
        c_ref[0, :, j * half:(j + 1) * half] = (ua * _sigmoid(ub)).astype(BF16)


def _proj(x, layer, gmix, wqkv_t, wu, qg, kg, rope_t):
    B, T, _ = x.shape
    tm = min(PROJ_TM, T)
    nt = T // tm
    grid = (B, nt)
    out_shape = (
        jax.ShapeDtypeStruct((B, Q_DIM, T), BF16),
        jax.ShapeDtypeStruct((B, N_KV_HEADS, T, HEAD_DIM), BF16),
        jax.ShapeDtypeStruct((B, nt, KV_DIM, tm), BF16),
        jax.ShapeDtypeStruct((B, T, CONV_DIM), BF16),
    )
    in_specs = [
        pl.BlockSpec((1, tm, D_MODEL), lambda b, i: (b, i, 0)),
        _const_spec((1, 1, D_MODEL), lambda b, i: (layer, 0, 0)),
        _const_spec((1, QKV_DIM, D_MODEL), lambda b, i: (layer, 0, 0)),
        _const_spec((1, D_MODEL, 2 * CONV_DIM), lambda b, i: (layer, 0, 0)),
        _const_spec((1, HEAD_DIM, 1), lambda b, i: (layer, 0, 0)),
        _const_spec((1, HEAD_DIM, 1), lambda b, i: (layer, 0, 0)),
        pl.BlockSpec((4 * HALF_AXIS, tm), lambda b, i: (0, i)),
    ]
    out_specs = (
        pl.BlockSpec((1, Q_DIM, tm), lambda b, i: (b, 0, i)),
        pl.BlockSpec((1, N_KV_HEADS, tm, HEAD_DIM), lambda b, i: (b, 0, i, 0)),
        pl.BlockSpec((1, 1, KV_DIM, tm), lambda b, i: (b, i, 0, 0)),
        pl.BlockSpec((1, tm, CONV_DIM), lambda b, i: (b, i, 0)),
    )
    return pl.pallas_call(
        _proj_kernel, grid=grid, in_specs=in_specs, out_specs=out_specs, out_shape=out_shape,
        compiler_params=pltpu.CompilerParams(
            dimension_semantics=("parallel", "parallel"), vmem_limit_bytes=VMEM_LIMIT),
        name="proj",
    )(x, gmix, wqkv_t, wu, qg, kg, rope_t)


def _attn_kernel(qt_ref, k_ref, vt_ref, o_ref, acc_ref, *, n_kv_blocks, ts):
    tq = qt_ref.shape[2]
    acc_ref[...] = jnp.zeros_like(acc_ref)
    neg = jnp.full((1, tq), -jnp.inf, F32)
    zero = jnp.zeros((1, tq), F32)

    def body(si, carry):
        ms, ls = carry
        kb = k_ref[0, 0, pl.ds(pl.multiple_of(si * ts, ts), ts), :]
        vb = vt_ref[0, si]
        new_ms, new_ls = [], []
        for g in range(GROUP):
            rows = slice(g * HEAD_DIM, (g + 1) * HEAD_DIM)
            s = _dot(kb, qt_ref[0, rows, :])
            m_new = jnp.maximum(ms[g], jnp.max(s, axis=0, keepdims=True))
            alpha = jnp.exp2(ms[g] - m_new)
            p = jnp.exp2(s - m_new)
            new_ls.append(alpha * ls[g] + jnp.sum(p, axis=0, keepdims=True))
            acc_ref[rows, :] = alpha * acc_ref[rows, :] + _dot(vb, p.astype(BF16))
            new_ms.append(m_new)
        return tuple(new_ms), tuple(new_ls)

    _, ls = lax.fori_loop(0, n_kv_blocks, body, ((neg,) * GROUP, (zero,) * GROUP))
    for g in range(GROUP):
        rows = slice(g * HEAD_DIM, (g + 1) * HEAD_DIM)
        acc_ref[rows, :] = acc_ref[rows, :] * (1.0 / ls[g])
    o_ref[0] = acc_ref[...].T.astype(BF16)


def _attention(q_t, k, v_t):
    B, _, T = q_t.shape
    n_s, ts = v_t.shape[1], v_t.shape[3]
    tq = min(ATTN_TQ, T)
    grid = (B, N_KV_HEADS, T // tq)
    gw = GROUP * HEAD_DIM
    return pl.pallas_call(
        functools.partial(_attn_kernel, n_kv_blocks=n_s, ts=ts),
        grid=grid,
        in_specs=[
            pl.BlockSpec((1, gw, tq), lambda b, h, i: (b, h, i)),
            pl.BlockSpec((1, 1, T, HEAD_DIM), lambda b, h, i: (b, h, 0, 0)),
            pl.BlockSpec((1, n_s, HEAD_DIM, ts), lambda b, h, i: (b, 0, h, 0)),
        ],
        out_specs=pl.BlockSpec((1, tq, gw), lambda b, h, i: (b, i, h)),
        out_shape=jax.ShapeDtypeStruct((B, T, Q_DIM), BF16),
        scratch_shapes=[pltpu.VMEM((gw, tq), F32)],
        compiler_params=pltpu.CompilerParams(
            dimension_semantics=("parallel", "parallel", "parallel"), vmem_limit_bytes=VMEM_LIMIT),
        name="attn",
    )(q_t, k, v_t)


def _mix_kernel(x_ref, cprev_ref, c_ref, cnext_ref, attn_ref, gmix_ref, wg_ref, wdw_ref, dwb_ref,
                lng_ref, lnb_ref, wco_ref, wo_ref, o_ref, sh_ref, cn_ref, *, n_tiles):
    i = pl.program_id(1)
    tm = c_ref.shape[1]
    span = tm + 3 * SUBLANES

    prev = cprev_ref[0].astype(F32)
    nxt = cnext_ref[0].astype(F32)
    sh_ref[0, 0:HALO, :] = jnp.where(i > 0, prev, 0.0)
    sh_ref[0, HALO:HALO + tm, :] = c_ref[0].astype(F32)
    sh_ref[0, HALO + tm:2 * HALO + tm, :] = jnp.where(i < n_tiles - 1, nxt, 0.0)
    for b in range(1, SUBLANES):
        sh_ref[b, 0:span, :] = sh_ref[0, b:b + span, :]

    dwb = dwb_ref[0]
    lng = lng_ref[0]
    lnb = lnb_ref[0]
    reps = MIX_ROWS // SUBLANES

    def conv_rows(rb, carry):
        r0 = pl.multiple_of(rb * MIX_ROWS, MIX_ROWS)
        acc = jnp.zeros((MIX_ROWS, CONV_DIM), F32)
        for kk in range(CONV_K):
            off = HALO - CONV_PAD + kk
            a, b = off // SUBLANES, off % SUBLANES
            w = jnp.concatenate([wdw_ref[0, kk]] * reps, axis=0)
            acc = acc + sh_ref[b, pl.ds(r0 + a * SUBLANES, MIX_ROWS), :] * w
        y = acc + dwb
        mu = jnp.mean(y, axis=-1, keepdims=True)
        yc = y - mu
        var = jnp.mean(yc * yc, axis=-1, keepdims=True)
        z = yc * lax.rsqrt(var + EPS) * lng + lnb
        cn_ref[pl.ds(r0, MIX_ROWS), :] = (z * _sigmoid(z)).astype(BF16)
        return carry

    lax.fori_loop(0, tm // MIX_ROWS, conv_rows, 0)

    x = x_ref[0]
    h = _rmsnorm_rows(x, gmix_ref[0]).astype(BF16)
    conv = _dot(cn_ref[...], wco_ref[0])
    sga = _sigmoid(_dot(h, wg_ref[0, :, 0:D_MODEL]))
    sgc = _sigmoid(_dot(h, wg_ref[0, :, D_MODEL:2 * D_MODEL]))
    merged = sga * attn_ref[0].astype(F32) + sgc * conv
    o_ref[0] = x + _dot(merged.astype(BF16), wo_ref[0])


def _mix(x, c, attn, layer, gmix, wg, wdw8, dwb, lng, lnb, wco, wo):
    B, T, _ = x.shape
    tm = min(MIX_TM, T)
    nt = T // tm
    hb = tm // HALO
    n_halo = T // HALO
    tile = lambda b, i: (b, i, 0)
    lay = lambda b, i: (layer, 0, 0)
    in_specs = [
        pl.BlockSpec((1, tm, D_MODEL), tile),
        pl.BlockSpec((1, HALO, CONV_DIM), lambda b, i: (b, jnp.maximum(i * hb - 1, 0), 0)),
        pl.BlockSpec((1, tm, CONV_DIM), tile),
        pl.BlockSpec((1, HALO, CONV_DIM), lambda b, i: (b, jnp.minimum((i + 1) * hb, n_halo - 1), 0)),
        pl.BlockSpec((1, tm, D_MODEL), tile),
        _const_spec((1, 1, D_MODEL), lay),
        _const_spec((1, D_MODEL, 2 * D_MODEL), lay),
        _const_spec((1, CONV_K, SUBLANES, CONV_DIM), lambda b, i: (layer, 0, 0, 0)),
        _const_spec((1, 1, CONV_DIM), lay),
        _const_spec((1, 1, CONV_DIM), lay),
        _const_spec((1, 1, CONV_DIM), lay),
        _const_spec((1, CONV_DIM, D_MODEL), lay),
        _const_spec((1, D_MODEL, D_MODEL), lay),
    ]
    return pl.pallas_call(
        functools.partial(_mix_kernel, n_tiles=nt),
        grid=(B, nt), in_specs=in_specs,
        out_specs=pl.BlockSpec((1, tm, D_MODEL), tile),
        out_shape=jax.ShapeDtypeStruct((B, T, D_MODEL), F32),
        scratch_shapes=[pltpu.VMEM((SUBLANES, tm + 2 * HALO, CONV_DIM), F32),
                        pltpu.VMEM((tm, CONV_DIM), BF16)],
        compiler_params=pltpu.CompilerParams(
            dimension_semantics=("parallel", "parallel"), vmem_limit_bytes=VMEM_LIMIT),
        name="mix",
    )(x, c, c, c, attn, gmix, wg, wdw8, dwb, lng, lnb, wco, wo)


def _mlp_kernel(x_ref, g_ref, wup_ref, wdn_ref, gfin_ref, o_ref, *, final):
    x = x_ref[...]
    h = _rmsnorm_rows(x, g_ref[0]).astype(BF16)
    y = x
    for j in range(D_FF // MLP_FC):
        cols = slice(j * MLP_FC, (j + 1) * MLP_FC)
        up = _dot(h, wup_ref[0, :, cols])
        act = jnp.square(jnp.maximum(up, 0.0)).astype(BF16)
        y = y + _dot(act, wdn_ref[0, cols, :])
    if final:
        y = _rmsnorm_rows(y, gfin_ref[...])
    o_ref[...] = y


def _mlp(x, layer, g, wup, wdn, gfin, final):
    B, T, _ = x.shape
    n = B * T
    tm = min(MLP_TM, n)
    x2 = x.reshape(n, D_MODEL)
    lay = lambda i: (layer, 0, 0)
    out = pl.pallas_call(
        functools.partial(_mlp_kernel, final=final),
        grid=(n // tm,),
        in_specs=[
            pl.BlockSpec((tm, D_MODEL), lambda i: (i, 0)),
            _const_spec((1, 1, D_MODEL), lay),
            _const_spec((1, D_MODEL, D_FF), lay),
            _const_spec((1, D_FF, D_MODEL), lay),
            _const_spec((1, D_MODEL), lambda i: (0, 0)),
        ],
        out_specs=pl.BlockSpec((tm, D_MODEL), lambda i: (i, 0)),
        out_shape=jax.ShapeDtypeStruct((n, D_MODEL), F32),
        compiler_params=pltpu.CompilerParams(
            dimension_semantics=("parallel",), vmem_limit_bytes=VMEM_LIMIT),
        name="mlp",
    )(x2, g, wup, wdn, gfin)
    return out.reshape(B, T, D_MODEL)


def _rope_table_t(T):
    rows = T // GRID_W
    row = jnp.repeat(jnp.arange(rows, dtype=F32), GRID_W)
    col = jnp.tile(jnp.arange(GRID_W, dtype=F32), rows)
    freqs = ROPE_THETA ** (-jnp.arange(0, AXIS_DIM, 2, dtype=F32) / AXIS_DIM)
    ang_r = row[:, None] * freqs[None, :]
    ang_c = col[:, None] * freqs[None, :]
    tab = jnp.concatenate([jnp.cos(ang_r), jnp.sin(ang_r), jnp.cos(ang_c), jnp.sin(ang_c)], axis=1)
    return tab.T


def _trunk(x, p):
    T = x.shape[1]
    rope_t = _rope_table_t(T)
    for l in range(DEPTH):
        q_t, k, v_t, c = _proj(x, l, p["gmix"], p["wqkv_t"], p["wu"], p["qg"], p["kg"], rope_t)
        attn = _attention(q_t, k, v_t)
        x = _mix(x, c, attn, l, p["gmix"], p["wg"], p["wdw8"], p["dwb"], p["lng"], p["lnb"],
                 p["wco"], p["wo"])
        x = _mlp(x, l, p["gmlp"], p["wup"], p["wdn"], p["gfin"], final=(l == DEPTH - 1))
    return x


def kernel(x_prompt, x_sample, g_mix, w_in, q_norm, k_norm, conv_dw, conv_dw_b, conv_ln_g,
           conv_ln_b, w_conv_out, w_o, g_mlp, w_up, w_down, g_final):
    u0 = QKV_DIM
    g0 = QKV_DIM + 2 * CONV_DIM
    p = {
        "gmix": g_mix.reshape(DEPTH, 1, D_MODEL),
        "wqkv_t": jnp.swapaxes(w_in[:, :, :u0], 1, 2).astype(BF16),
        "wu": w_in[:, :, u0:g0].astype(BF16),
        "wg": w_in[:, :, g0:].astype(BF16),
        "qg": q_norm.reshape(DEPTH, HEAD_DIM, 1),
        "kg": k_norm.reshape(DEPTH, HEAD_DIM, 1),
        "wdw8": jnp.broadcast_to(conv_dw[:, :, None, :], (DEPTH, CONV_K, SUBLANES, CONV_DIM)),
        "dwb": conv_dw_b.reshape(DEPTH, 1, CONV_DIM),
        "lng": conv_ln_g.reshape(DEPTH, 1, CONV_DIM),
        "lnb": conv_ln_b.reshape(DEPTH, 1, CONV_DIM),
        "wco": w_conv_out.astype(BF16),
        "wo": w_o.astype(BF16),
        "gmlp": g_mlp.reshape(DEPTH, 1, D_MODEL),
        "wup": w_up.astype(BF16),
        "wdn": w_down.astype(BF16),
        "gfin": g_final.reshape(1, D_MODEL),
    }
    return (_trunk(x_prompt, p), _trunk(x_sample, p))
```

```python
import functools
import math

import jax
import jax.numpy as jnp
from jax import lax
from jax.experimental import pallas as pl
from jax.experimental.pallas import tpu as pltpu

D_MODEL = 1024
DEPTH = 4
HEAD_DIM = 64
N_Q_HEADS = 16
N_KV_HEADS = 4
GROUP = N_Q_HEADS // N_KV_HEADS
Q_DIM = N_Q_HEADS * HEAD_DIM
KV_DIM = N_KV_HEADS * HEAD_DIM
QKV_DIM = Q_DIM + 2 * KV_DIM
ROPE_THETA = 10000.0
AXIS_DIM = HEAD_DIM // 2
HALF_AXIS = AXIS_DIM // 2
GRID_W = 64
CONV_DIM = D_MODEL
CONV_K = 31
CONV_PAD = (CONV_K - 1) // 2
D_FF = 4 * D_MODEL
EPS = 1e-6

SUBLANES = 8
BF16_SUBLANES = 16
HALO = BF16_SUBLANES
ACC_ROWS = HEAD_DIM + BF16_SUBLANES
VMEM_LIMIT = 56 * 1024 * 1024

Q_SCALE = math.log2(math.e) / math.sqrt(HEAD_DIM)

BF16 = jnp.bfloat16
F32 = jnp.float32

PROJ_TM = 512
ATTN_TQ = 512
MIX_TM = 256
MIX_ROWS = 32
MLP_TM = 512
MLP_FC = 1024


def _dot(a, b):
    return jnp.dot(a, b, preferred_element_type=F32)


def _dot_nt(a, b):
    return lax.dot_general(a, b, (((1,), (1,)), ((), ())), preferred_element_type=F32)


def _sigmoid(x):
    return 1.0 / (1.0 + jnp.exp(-x))


def _rmsnorm_rows(x, g):
    ms = jnp.mean(x * x, axis=-1, keepdims=True)
    return x * lax.rsqrt(ms + EPS) * g


def _const_spec(shape, index_map):
    return pl.BlockSpec(shape, index_map, pipeline_mode=pl.Buffered(1))


def _proj_kernel(x_ref, gmix_ref, wqkv_ref, wu_ref, qg_ref, kg_ref, rope_ref,
                 qt_ref, k_ref, vt_ref, c_ref):
    x = x_ref[0]
    h = _rmsnorm_rows(x, gmix_ref[0]).astype(BF16)

    qkv_t = _dot_nt(wqkv_ref[0], h)
    rope = rope_ref[...]
    cr, sr = rope[0:HALF_AXIS], rope[HALF_AXIS:2 * HALF_AXIS]
    cc, sc = rope[2 * HALF_AXIS:3 * HALF_AXIS], rope[3 * HALF_AXIS:4 * HALF_AXIS]

    def norm_rope(a, g):
        r = lax.rsqrt(jnp.mean(a * a, axis=0, keepdims=True) + EPS)
        an = a * r * g
        x1r, x2r = an[0:HALF_AXIS], an[HALF_AXIS:2 * HALF_AXIS]
        x1c, x2c = an[2 * HALF_AXIS:3 * HALF_AXIS], an[3 * HALF_AXIS:4 * HALF_AXIS]
        return jnp.concatenate(
            [x1r * cr - x2r * sr, x2r * cr + x1r * sr,
             x1c * cc - x2c * sc, x2c * cc + x1c * sc], axis=0)

    qg = qg_ref[0] * Q_SCALE
    kg = kg_ref[0]
    for hh in range(N_Q_HEADS):
        a = qkv_t[hh * HEAD_DIM:(hh + 1) * HEAD_DIM]
        qt_ref[0, hh * HEAD_DIM:(hh + 1) * HEAD_DIM, :] = norm_rope(a, qg).astype(BF16)
    for hh in range(N_KV_HEADS):
        a = qkv_t[Q_DIM + hh * HEAD_DIM:Q_DIM + (hh + 1) * HEAD_DIM]
        k_ref[0, hh] = norm_rope(a, kg).T.astype(BF16)
    vt_ref[0, 0] = qkv_t[Q_DIM + KV_DIM:].astype(BF16)

    half = CONV_DIM // 2
    for j in range(2):
        ua = _dot(h, wu_ref[0, :, j * half:(j + 1) * half])
        ub = _dot(h, wu_ref[0, :, CONV_DIM + j * half:CONV_DIM + (j + 1) * half])
        c_ref[0, :, j * half:(j + 1) * half] = (ua * _sigmoid(ub)).astype(BF16)


def _proj(x, layer, gmix, wqkv_t, wu, qg, kg, rope_t):
    B, T, _ = x.shape
    tm = min(PROJ_TM, T)
    nt = T // tm
    grid = (B, nt)
    out_shape = (
        jax.ShapeDtypeStruct((B, Q_DIM, T), BF16),
        jax.ShapeDtypeStruct((B, N_KV_HEADS, T, HEAD_DIM), BF16),
        jax.ShapeDtypeStruct((B, nt, KV_DIM, tm), BF16),
        jax.ShapeDtypeStruct((B, T, CONV_DIM), BF16),
    )
    in_specs = [
        pl.BlockSpec((1, tm, D_MODEL), lambda b, i: (b, i, 0)),
        _const_spec((1, 1, D_MODEL), lambda b, i: (layer, 0, 0)),
        _const_spec((1, QKV_DIM, D_MODEL), lambda b, i: (layer, 0, 0)),
        _const_spec((1, D_MODEL, 2 * CONV_DIM), lambda b, i: (layer, 0, 0)),
        _const_spec((1, HEAD_DIM, 1), lambda b, i: (layer, 0, 0)),
        _const_spec((1, HEAD_DIM, 1), lambda b, i: (layer, 0, 0)),
        pl.BlockSpec((4 * HALF_AXIS, tm), lambda b, i: (0, i)),
    ]
    out_specs = (
        pl.BlockSpec((1, Q_DIM, tm), lambda b, i: (b, 0, i)),
        pl.BlockSpec((1, N_KV_HEADS, tm, HEAD_DIM), lambda b, i: (b, 0, i, 0)),
        pl.BlockSpec((1, 1, KV_DIM, tm), lambda b, i: (b, i, 0, 0)),
        pl.BlockSpec((1, tm, CONV_DIM), lambda b, i: (b, i, 0)),
    )
    return pl.pallas_call(
        _proj_kernel, grid=grid, in_specs=in_specs, out_specs=out_specs, out_shape=out_shape,
        compiler_params=pltpu.CompilerParams(
            dimension_semantics=("parallel", "parallel"), vmem_limit_bytes=VMEM_LIMIT),
        name="proj",
    )(x, gmix, wqkv_t, wu, qg, kg, rope_t)


def _attn_kernel(qt_ref, k_ref, vt_ref, o_ref, acc_ref, m_ref, s_ref, mc_ref, p_ref, al_ref,
                 *, n_kv_blocks, ts):
    acc_ref[...] = jnp.zeros_like(acc_ref)
    m_ref[...] = jnp.full_like(m_ref, -jnp.inf)
    hs = ts // 2
    n_steps = 2 * n_kv_blocks
    ones = jnp.ones((ACC_ROWS - HEAD_DIM, hs), BF16)

    def k_half(si, h):
        return k_ref[0, 0, pl.ds(pl.multiple_of(si * ts + h * hs, hs), hs), :]

    def v_half(si, h):
        return jnp.concatenate([vt_ref[0, si, :, h * hs:(h + 1) * hs], ones], axis=0)

    def scores(kb, g, slot):
        rows = slice(g * HEAD_DIM, (g + 1) * HEAD_DIM)
        s = _dot(kb, qt_ref[0, rows, :])
        s_ref[slot, g] = s
        mc_ref[slot, g] = jnp.max(s, axis=0, keepdims=True)

    def softmax(g, slot):
        m_prev = m_ref[g]
        m_new = jnp.maximum(m_prev, mc_ref[slot, g])
        al_ref[slot, g] = jnp.exp2(m_prev - m_new)
        p_ref[slot, g] = jnp.exp2(s_ref[slot, g] - m_new).astype(BF16)
        m_ref[g] = m_new

    def pv(vb, g, slot):
        acc_ref[g] = al_ref[slot, g] * acc_ref[g] + _dot(vb, p_ref[slot, g])

    def step(kb, sc_slot, sm_slot, vb, pv_slot):
        for g in range(GROUP):
            if kb is not None:
                scores(kb, g, sc_slot)
            if sm_slot is not None:
                softmax(g, sm_slot)
            if vb is not None:
                pv(vb, g, pv_slot)

    step(k_half(0, 0), 0, None, None, None)
    step(k_half(0, 1), 1, None, None, None)
    step(k_half(1, 0), 2, 0, None, None)
    step(k_half(1, 1), 3, 1, None, None)

    def two_steps(j, e):
        step(k_half(j, 0), e, 2 - e, v_half(j - 2, 0), e)
        step(k_half(j, 1), e + 1, 3 - e, v_half(j - 2, 1), e + 1)

    def body(j, carry):
        lax.cond(lax.rem(j, 2) == 0, lambda: two_steps(j, 0), lambda: two_steps(j, 2))
        return carry

    lax.fori_loop(2, n_kv_blocks, body, 0)
    for t in range(n_steps, n_steps + 4):
        sm = (t - 2) % 4 if t - 2 < n_steps else None
        u = t - 4
        step(None, None, sm, v_half(u // 2, u % 2), u % 4)
    outs = []
    for g in range(GROUP):
        a = acc_ref[g]
        outs.append(a[0:HEAD_DIM] * (1.0 / a[HEAD_DIM:HEAD_DIM + 1]))
    o_ref[0] = jnp.concatenate(outs, axis=0).T.astype(BF16)


def _attention(q_t, k, v_t):
    B, _, T = q_t.shape
    n_s, ts = v_t.shape[1], v_t.shape[3]
    tq = min(ATTN_TQ, T)
    grid = (B, N_KV_HEADS, T // tq)
    gw = GROUP * HEAD_DIM
    assert n_s >= 2, "the pipeline fill covers two kv blocks"
    slots = 4
    stat = pltpu.VMEM((slots, GROUP, 1, tq), F32)
    return pl.pallas_call(
        functools.partial(_attn_kernel, n_kv_blocks=n_s, ts=ts),
        grid=grid,
        in_specs=[
            pl.BlockSpec((1, gw, tq), lambda b, h, i: (b, h, i)),
            pl.BlockSpec((1, 1, T, HEAD_DIM), lambda b, h, i: (b, h, 0, 0)),
            pl.BlockSpec((1, n_s, HEAD_DIM, ts), lambda b, h, i: (b, 0, h, 0)),
        ],
        out_specs=pl.BlockSpec((1, tq, gw), lambda b, h, i: (b, i, h)),
        out_shape=jax.ShapeDtypeStruct((B, T, Q_DIM), BF16),
        scratch_shapes=[pltpu.VMEM((GROUP, ACC_ROWS, tq), F32),
                        pltpu.VMEM((GROUP, 1, tq), F32),
                        pltpu.VMEM((slots, GROUP, ts // 2, tq), F32), stat,
                        pltpu.VMEM((slots, GROUP, ts // 2, tq), BF16), stat],
        compiler_params=pltpu.CompilerParams(
            dimension_semantics=("parallel", "parallel", "parallel"), vmem_limit_bytes=VMEM_LIMIT),
        name="attn",
    )(q_t, k, v_t)


def _mix_kernel(x_ref, cprev_ref, c_ref, cnext_ref, attn_ref, gmix_ref, wg_ref, wdw_ref, dwb_ref,
                lng_ref, lnb_ref, wco_ref, wo_ref, o_ref, sh_ref, cn_ref, *, n_tiles):
    i = pl.program_id(1)
    tm = c_ref.shape[1]
    span = tm + 3 * SUBLANES

    prev = cprev_ref[0].astype(F32)
    nxt = cnext_ref[0].astype(F32)
    sh_ref[0, 0:HALO, :] = jnp.where(i > 0, prev, 0.0)
    sh_ref[0, HALO:HALO + tm, :] = c_ref[0].astype(F32)
    sh_ref[0, HALO + tm:2 * HALO + tm, :] = jnp.where(i < n_tiles - 1, nxt, 0.0)
    for b in range(1, SUBLANES):
        sh_ref[b, 0:span, :] = sh_ref[0, b:b + span, :]

    dwb = dwb_ref[0]
    lng = lng_ref[0]
    lnb = lnb_ref[0]
    reps = MIX_ROWS // SUBLANES

    def conv_rows(rb, carry):
        r0 = pl.multiple_of(rb * MIX_ROWS, MIX_ROWS)
        halves = []
        for lanes in (slice(0, CONV_DIM // 2), slice(CONV_DIM // 2, CONV_DIM)):
            accs = [jnp.zeros((SUBLANES, CONV_DIM // 2), F32) for _ in range(reps)]
            first = HALO - CONV_PAD
            for b in range(SUBLANES):
                taps = [off - first for off in range(b, first + CONV_K, SUBLANES)
                        if 0 <= off - first < CONV_K]
                groups = {}
                for kk in taps:
                    a = (first + kk) // SUBLANES
                    w = wdw_ref[0, kk, :, lanes]
                    for r in range(reps):
                        if a + r not in groups:
                            rows = pl.ds(r0 + (a + r) * SUBLANES, SUBLANES)
                            groups[a + r] = sh_ref[b, rows, lanes]
                        accs[r] = accs[r] + groups[a + r] * w
            halves.append(jnp.concatenate(accs, axis=0))
        y = jnp.concatenate(halves, axis=1) + dwb
        mu = jnp.mean(y, axis=-1, keepdims=True)
        yc = y - mu
        var = jnp.mean(yc * yc, axis=-1, keepdims=True)
        z = yc * lax.rsqrt(var + EPS) * lng + lnb
        cn_ref[pl.ds(r0, MIX_ROWS), :] = (z * _sigmoid(z)).astype(BF16)
        return carry

    lax.fori_loop(0, tm // MIX_ROWS, conv_rows, 0)

    x = x_ref[0]
    h = _rmsnorm_rows(x, gmix_ref[0]).astype(BF16)
    conv = _dot(cn_ref[...], wco_ref[0])
    sga = _sigmoid(_dot(h, wg_ref[0, :, 0:D_MODEL]))
    sgc = _sigmoid(_dot(h, wg_ref[0, :, D_MODEL:2 * D_MODEL]))
    merged = sga * attn_ref[0].astype(F32) + sgc * conv
    o_ref[0] = x + _dot(merged.astype(BF16), wo_ref[0])


def _mix(x, c, attn, layer, gmix, wg, wdw8, dwb, lng, lnb, wco, wo):
    B, T, _ = x.shape
    tm = min(MIX_TM, T)
    nt = T // tm
    hb = tm // HALO
    n_halo = T // HALO
    tile = lambda b, i: (b, i, 0)
    lay = lambda b, i: (layer, 0, 0)
    in_specs = [
        pl.BlockSpec((1, tm, D_MODEL), tile),
        pl.BlockSpec((1, HALO, CONV_DIM), lambda b, i: (b, jnp.maximum(i * hb - 1, 0), 0)),
        pl.BlockSpec((1, tm, CONV_DIM), tile),
        pl.BlockSpec((1, HALO, CONV_DIM), lambda b, i: (b, jnp.minimum((i + 1) * hb, n_halo - 1), 0)),
        pl.BlockSpec((1, tm, D_MODEL), tile),
        _const_spec((1, 1, D_MODEL), lay),
        _const_spec((1, D_MODEL, 2 * D_MODEL), lay),
        _const_spec((1, CONV_K, SUBLANES, CONV_DIM), lambda b, i: (layer, 0, 0, 0)),
        _const_spec((1, 1, CONV_DIM), lay),
        _const_spec((1, 1, CONV_DIM), lay),
        _const_spec((1, 1, CONV_DIM), lay),
        _const_spec((1, CONV_DIM, D_MODEL), lay),
        _const_spec((1, D_MODEL, D_MODEL), lay),
    ]
    return pl.pallas_call(
        functools.partial(_mix_kernel, n_tiles=nt),
        grid=(B, nt), in_specs=in_specs,
        out_specs=pl.BlockSpec((1, tm, D_MODEL), tile),
        out_shape=jax.ShapeDtypeStruct((B, T, D_MODEL), F32),
        scratch_shapes=[pltpu.VMEM((SUBLANES, tm + 2 * HALO, CONV_DIM), F32),
                        pltpu.VMEM((tm, CONV_DIM), BF16)],
        compiler_params=pltpu.CompilerParams(
            dimension_semantics=("parallel", "parallel"), vmem_limit_bytes=VMEM_LIMIT),
        name="mix",
    )(x, c, c, c, attn, gmix, wg, wdw8, dwb, lng, lnb, wco, wo)


def _mlp_kernel(x_ref, g_ref, wup_ref, wdn_ref, gfin_ref, o_ref, *, final):
    x = x_ref[...]
    h = _rmsnorm_rows(x, g_ref[0]).astype(BF16)
    y = x
    for j in range(D_FF // MLP_FC):
        cols = slice(j * MLP_FC, (j + 1) * MLP_FC)
        up = _dot(h, wup_ref[0, :, cols])
        act = jnp.square(jnp.maximum(up, 0.0)).astype(BF16)
        y = y + _dot(act, wdn_ref[0, cols, :])
    if final:
        y = _rmsnorm_rows(y, gfin_ref[...])
    o_ref[...] = y


def _mlp(x, layer, g, wup, wdn, gfin, final):
    B, T, _ = x.shape
    n = B * T
    tm = min(MLP_TM, n)
    x2 = x.reshape(n, D_MODEL)
    lay = lambda i: (layer, 0, 0)
    out = pl.pallas_call(
        functools.partial(_mlp_kernel, final=final),
        grid=(n // tm,),
        in_specs=[
            pl.BlockSpec((tm, D_MODEL), lambda i: (i, 0)),
            _const_spec((1, 1, D_MODEL), lay),
            _const_spec((1, D_MODEL, D_FF), lay),
            _const_spec((1, D_FF, D_MODEL), lay),
            _const_spec((1, D_MODEL), lambda i: (0, 0)),
        ],
        out_specs=pl.BlockSpec((tm, D_MODEL), lambda i: (i, 0)),
        out_shape=jax.ShapeDtypeStruct((n, D_MODEL), F32),
        compiler_params=pltpu.CompilerParams(
            dimension_semantics=("parallel",), vmem_limit_bytes=VMEM_LIMIT),
        name="mlp",
    )(x2, g, wup, wdn, gfin)
    return out.reshape(B, T, D_MODEL)


def _rope_table_t(T):
    rows = T // GRID_W
    row = jnp.repeat(jnp.arange(rows, dtype=F32), GRID_W)
    col = jnp.tile(jnp.arange(GRID_W, dtype=F32), rows)
    freqs = ROPE_THETA ** (-jnp.arange(0, AXIS_DIM, 2, dtype=F32) / AXIS_DIM)
    ang_r = row[:, None] * freqs[None, :]
    ang_c = col[:, None] * freqs[None, :]
    tab = jnp.concatenate([jnp.cos(ang_r), jnp.sin(ang_r), jnp.cos(ang_c), jnp.sin(ang_c)], axis=1)
    return tab.T


def _trunk(x, p):
    T = x.shape[1]
    rope_t = _rope_table_t(T)
    for l in range(DEPTH):
        q_t, k, v_t, c = _proj(x, l, p["gmix"], p["wqkv_t"], p["wu"], p["qg"], p["kg"], rope_t)
        attn = _attention(q_t, k, v_t)
        x = _mix(x, c, attn, l, p["gmix"], p["wg"], p["wdw8"], p["dwb"], p["lng"], p["lnb"],
                 p["wco"], p["wo"])
        x = _mlp(x, l, p["gmlp"], p["wup"], p["wdn"], p["gfin"], final=(l == DEPTH - 1))
    return x


def kernel(x_prompt, x_sample, g_mix, w_in, q_norm, k_norm, conv_dw, conv_dw_b, conv_ln_g,
           conv_ln_b, w_conv_out, w_o, g_mlp, w_up, w_down, g_final):
    u0 = QKV_DIM
    g0 = QKV_DIM + 2 * CONV_DIM
    p = {
        "gmix": g_mix.reshape(DEPTH, 1, D_MODEL),
        "wqkv_t": jnp.swapaxes(w_in[:, :, :u0], 1, 2).astype(BF16),
        "wu": w_in[:, :, u0:g0].astype(BF16),
        "wg": w_in[:, :, g0:].astype(BF16),
        "qg": q_norm.reshape(DEPTH, HEAD_DIM, 1),
        "kg": k_norm.reshape(DEPTH, HEAD_DIM, 1),
        "wdw8": jnp.broadcast_to(conv_dw[:, :, None, :], (DEPTH, CONV_K, SUBLANES, CONV_DIM)),
        "dwb": conv_dw_b.reshape(DEPTH, 1, CONV_DIM),
        "lng": conv_ln_g.reshape(DEPTH, 1, CONV_DIM),
        "lnb": conv_ln_b.reshape(DEPTH, 1, CONV_DIM),
        "wco": w_conv_out.astype(BF16),
        "wo": w_o.astype(BF16),
        "gmlp": g_mlp.reshape(DEPTH, 1, D_MODEL),
        "wup": w_up.astype(BF16),
        "wdn": w_down.astype(BF16),
        "gfin": g_final.reshape(1, D_MODEL),
    }
    return (_trunk(x_prompt, p), _trunk(x_sample, p))
```

```python
import functools
import math

import jax
import jax.numpy as jnp
from jax import lax
from jax.experimental import pallas as pl
from jax.experimental.pallas import tpu as pltpu

D_MODEL = 1024
DEPTH = 4
HEAD_DIM = 64
N_Q_HEADS = 16
N_KV_HEADS = 4
GROUP = N_Q_HEADS // N_KV_HEADS
Q_DIM = N_Q_HEADS * HEAD_DIM
KV_DIM = N_KV_HEADS * HEAD_DIM
QKV_DIM = Q_DIM + 2 * KV_DIM
ROPE_THETA = 10000.0
AXIS_DIM = HEAD_DIM // 2
HALF_AXIS = AXIS_DIM // 2
GRID_W = 64
CONV_DIM = D_MODEL
CONV_K = 31
CONV_PAD = (CONV_K - 1) // 2
D_FF = 4 * D_MODEL
EPS = 1e-6

SUBLANES = 8
LANES = 128
BF16_SUBLANES = 16
HALO = BF16_SUBLANES
ACC_ROWS = HEAD_DIM + BF16_SUBLANES
VMEM_LIMIT = 56 * 1024 * 1024

Q_SCALE = math.log2(math.e) / math.sqrt(HEAD_DIM)

BF16 = jnp.bfloat16
F32 = jnp.float32

PROJ_TM = 512
ATTN_TQ = 512
MIX_TM = 256
MIX_ROWS = 64
MIX_LANES = LANES
MLP_TM = 512
MLP_FC = 1024


def _dot(a, b):
    return jnp.dot(a, b, preferred_element_type=F32)


def _dot_nt(a, b):
    return lax.dot_general(a, b, (((1,), (1,)), ((), ())), preferred_element_type=F32)


def _sigmoid(x):
    return 1.0 / (1.0 + jnp.exp(-x))


def _rmsnorm_rows(x, g):
    ms = jnp.mean(x * x, axis=-1, keepdims=True)
    return x * lax.rsqrt(ms + EPS) * g


def _const_spec(shape, index_map):
    return pl.BlockSpec(shape, index_map, pipeline_mode=pl.Buffered(1))


def _proj_kernel(x_ref, gmix_ref, wqkv_ref, wu_ref, qg_ref, kg_ref, rope_ref,
                 qt_ref, k_ref, vt_ref, c_ref):
    x = x_ref[0]
    h = _rmsnorm_rows(x, gmix_ref[0]).astype(BF16)

    qkv_t = _dot_nt(wqkv_ref[0], h)
    rope = rope_ref[...]
    cr, sr = rope[0:HALF_AXIS], rope[HALF_AXIS:2 * HALF_AXIS]
    cc, sc = rope[2 * HALF_AXIS:3 * HALF_AXIS], rope[3 * HALF_AXIS:4 * HALF_AXIS]

    def norm_rope(a, g):
        r = lax.rsqrt(jnp.mean(a * a, axis=0, keepdims=True) + EPS)
        an = a * r * g
        x1r, x2r = an[0:HALF_AXIS], an[HALF_AXIS:2 * HALF_AXIS]
        x1c, x2c = an[2 * HALF_AXIS:3 * HALF_AXIS], an[3 * HALF_AXIS:4 * HALF_AXIS]
        return jnp.concatenate(
            [x1r * cr - x2r * sr, x2r * cr + x1r * sr,
             x1c * cc - x2c * sc, x2c * cc + x1c * sc], axis=0)

    qg = qg_ref[0] * Q_SCALE
    kg = kg_ref[0]
    for hh in range(N_Q_HEADS):
        a = qkv_t[hh * HEAD_DIM:(hh + 1) * HEAD_DIM]
        qt_ref[0, hh * HEAD_DIM:(hh + 1) * HEAD_DIM, :] = norm_rope(a, qg).astype(BF16)
    for hh in range(N_KV_HEADS):
        a = qkv_t[Q_DIM + hh * HEAD_DIM:Q_DIM + (hh + 1) * HEAD_DIM]
        k_ref[0, hh] = norm_rope(a, kg).T.astype(BF16)
    vt_ref[0, 0] = qkv_t[Q_DIM + KV_DIM:].astype(BF16)

    half = CONV_DIM // 2
    for j in range(2):
        ua = _dot(h, wu_ref[0, :, j * half:(j + 1) * half])
        ub = _dot(h, wu_ref[0, :, CONV_DIM + j * half:CONV_DIM + (j + 1) * half])
        c_ref[0, :, j * half:(j + 1) * half] = (ua * _sigmoid(ub)).astype(BF16)


def _proj(x, layer, gmix, wqkv_t, wu, qg, kg, rope_t):
    B, T, _ = x.shape
    tm = min(PROJ_TM, T)
    nt = T // tm
    grid = (B, nt)
    out_shape = (
        jax.ShapeDtypeStruct((B, Q_DIM, T), BF16),
        jax.ShapeDtypeStruct((B, N_KV_HEADS, T, HEAD_DIM), BF16),
        jax.ShapeDtypeStruct((B, nt, KV_DIM, tm), BF16),
        jax.ShapeDtypeStruct((B, T, CONV_DIM), BF16),
    )
    in_specs = [
        pl.BlockSpec((1, tm, D_MODEL), lambda b, i: (b, i, 0)),
        _const_spec((1, 1, D_MODEL), lambda b, i: (layer, 0, 0)),
        _const_spec((1, QKV_DIM, D_MODEL), lambda b, i: (layer, 0, 0)),
        _const_spec((1, D_MODEL, 2 * CONV_DIM), lambda b, i: (layer, 0, 0)),
        _const_spec((1, HEAD_DIM, 1), lambda b, i: (layer, 0, 0)),
        _const_spec((1, HEAD_DIM, 1), lambda b, i: (layer, 0, 0)),
        pl.BlockSpec((4 * HALF_AXIS, tm), lambda b, i: (0, i)),
    ]
    out_specs = (
        pl.BlockSpec((1, Q_DIM, tm), lambda b, i: (b, 0, i)),
        pl.BlockSpec((1, N_KV_HEADS, tm, HEAD_DIM), lambda b, i: (b, 0, i, 0)),
        pl.BlockSpec((1, 1, KV_DIM, tm), lambda b, i: (b, i, 0, 0)),
        pl.BlockSpec((1, tm, CONV_DIM), lambda b, i: (b, i, 0)),
    )
    return pl.pallas_call(
        _proj_kernel, grid=grid, in_specs=in_specs, out_specs=out_specs, out_shape=out_shape,
        compiler_params=pltpu.CompilerParams(
            dimension_semantics=("parallel", "parallel"), vmem_limit_bytes=VMEM_LIMIT),
        name="proj",
    )(x, gmix, wqkv_t, wu, qg, kg, rope_t)


def _attn_kernel(qt_ref, k_ref, vt_ref, o_ref, acc_ref, m_ref, s_ref, mc_ref, p_ref, al_ref,
                 *, n_kv_blocks, ts):
    acc_ref[...] = jnp.zeros_like(acc_ref)
    m_ref[...] = jnp.full_like(m_ref, -jnp.inf)
    hs = ts // 2
    n_steps = 2 * n_kv_blocks
    ones = jnp.ones((ACC_ROWS - HEAD_DIM, hs), BF16)

    def k_half(si, h):
        return k_ref[0, 0, pl.ds(pl.multiple_of(si * ts + h * hs, hs), hs), :]

    def v_half(si, h):
        return jnp.concatenate([vt_ref[0, si, :, h * hs:(h + 1) * hs], ones], axis=0)

    def scores(kb, g, slot):
        rows = slice(g * HEAD_DIM, (g + 1) * HEAD_DIM)
        s = _dot(kb, qt_ref[0, rows, :])
        s_ref[slot, g] = s
        mc_ref[slot, g] = jnp.max(s, axis=0, keepdims=True)

    def softmax(g, slot):
        m_prev = m_ref[g]
        m_new = jnp.maximum(m_prev, mc_ref[slot, g])
        al_ref[slot, g] = jnp.exp2(m_prev - m_new)
        p_ref[slot, g] = jnp.exp2(s_ref[slot, g] - m_new).astype(BF16)
        m_ref[g] = m_new

    def pv(vb, g, slot):
        acc_ref[g] = al_ref[slot, g] * acc_ref[g] + _dot(vb, p_ref[slot, g])

    def step(kb, sc_slot, sm_slot, vb, pv_slot):
        for g in range(GROUP):
            if sm_slot is not None:
                softmax(g, sm_slot)
            if kb is not None:
                scores(kb, g, sc_slot)
            if vb is not None:
                pv(vb, g, pv_slot)

    step(k_half(0, 0), 0, None, None, None)
    step(k_half(0, 1), 1, None, None, None)
    step(k_half(1, 0), 2, 0, None, None)
    step(k_half(1, 1), 3, 1, None, None)

    def two_steps(j, e):
        step(k_half(j, 0), e, 2 - e, v_half(j - 2, 0), e)
        step(k_half(j, 1), e + 1, 3 - e, v_half(j - 2, 1), e + 1)

    def body(j, carry):
        lax.cond(lax.rem(j, 2) == 0, lambda: two_steps(j, 0), lambda: two_steps(j, 2))
        return carry

    lax.fori_loop(2, n_kv_blocks, body, 0)
    for t in range(n_steps, n_steps + 4):
        sm = (t - 2) % 4 if t - 2 < n_steps else None
        u = t - 4
        step(None, None, sm, v_half(u // 2, u % 2), u % 4)
    outs = []
    for g in range(GROUP):
        a = acc_ref[g]
        outs.append(a[0:HEAD_DIM] * (1.0 / a[HEAD_DIM:HEAD_DIM + 1]))
    o_ref[0] = jnp.concatenate(outs, axis=0).T.astype(BF16)


def _attention(q_t, k, v_t):
    B, _, T = q_t.shape
    n_s, ts = v_t.shape[1], v_t.shape[3]
    tq = min(ATTN_TQ, T)
    grid = (B, N_KV_HEADS, T // tq)
    gw = GROUP * HEAD_DIM
    assert n_s >= 2, "the pipeline fill covers two kv blocks"
    slots = 4
    stat = pltpu.VMEM((slots, GROUP, 1, tq), F32)
    return pl.pallas_call(
        functools.partial(_attn_kernel, n_kv_blocks=n_s, ts=ts),
        grid=grid,
        in_specs=[
            pl.BlockSpec((1, gw, tq), lambda b, h, i: (b, h, i)),
            pl.BlockSpec((1, 1, T, HEAD_DIM), lambda b, h, i: (b, h, 0, 0)),
            pl.BlockSpec((1, n_s, HEAD_DIM, ts), lambda b, h, i: (b, 0, h, 0)),
        ],
        out_specs=pl.BlockSpec((1, tq, gw), lambda b, h, i: (b, i, h)),
        out_shape=jax.ShapeDtypeStruct((B, T, Q_DIM), BF16),
        scratch_shapes=[pltpu.VMEM((GROUP, ACC_ROWS, tq), F32),
                        pltpu.VMEM((GROUP, 1, tq), F32),
                        pltpu.VMEM((slots, GROUP, ts // 2, tq), F32), stat,
                        pltpu.VMEM((slots, GROUP, ts // 2, tq), BF16), stat],
        compiler_params=pltpu.CompilerParams(
            dimension_semantics=("parallel", "parallel", "parallel"), vmem_limit_bytes=VMEM_LIMIT),
        name="attn",
    )(q_t, k, v_t)


def _mix_kernel(x_ref, cprev_ref, c_ref, cnext_ref, attn_ref, gmix_ref, wg_ref, wdw_ref, dwb_ref,
                lng_ref, lnb_ref, wco_ref, wo_ref, o_ref, sh_ref, cn_ref, *, n_tiles):
    i = pl.program_id(1)
    tm = c_ref.shape[1]
    span = tm + 3 * SUBLANES

    prev = cprev_ref[0].astype(F32)
    nxt = cnext_ref[0].astype(F32)
    sh_ref[0, 0:HALO, :] = jnp.where(i > 0, prev, 0.0)
    sh_ref[0, HALO:HALO + tm, :] = c_ref[0].astype(F32)
    sh_ref[0, HALO + tm:2 * HALO + tm, :] = jnp.where(i < n_tiles - 1, nxt, 0.0)
    for b in range(1, SUBLANES):
        sh_ref[b, 0:span, :] = sh_ref[0, b:b + span, :]

    dwb = dwb_ref[0]
    lng = lng_ref[0]
    lnb = lnb_ref[0]
    reps = MIX_ROWS // SUBLANES
    first = HALO - CONV_PAD

    def conv_rows(rb, carry):
        r0 = pl.multiple_of(rb * MIX_ROWS, MIX_ROWS)
        chunks = []
        for lc in range(CONV_DIM // MIX_LANES):
            lanes = slice(lc * MIX_LANES, (lc + 1) * MIX_LANES)
            accs = [jnp.zeros((SUBLANES, MIX_LANES), F32) for _ in range(reps)]
            for b in range(SUBLANES):
                taps = [off - first for off in range(b, first + CONV_K, SUBLANES)
                        if 0 <= off - first < CONV_K]
                groups = {}
                for kk in taps:
                    a = (first + kk) // SUBLANES
                    w = wdw_ref[0, kk, :, lanes]
                    for r in range(reps):
                        if a + r not in groups:
                            rows = pl.ds(r0 + (a + r) * SUBLANES, SUBLANES)
                            groups[a + r] = sh_ref[b, rows, lanes]
                        accs[r] = accs[r] + groups[a + r] * w
            chunks.append(jnp.concatenate(accs, axis=0))
        y = jnp.concatenate(chunks, axis=1) + dwb
        mu = jnp.mean(y, axis=-1, keepdims=True)
        yc = y - mu
        var = jnp.mean(yc * yc, axis=-1, keepdims=True)
        z = yc * lax.rsqrt(var + EPS) * lng + lnb
        cn_ref[pl.ds(r0, MIX_ROWS), :] = (z * _sigmoid(z)).astype(BF16)
        return carry

    lax.fori_loop(0, tm // MIX_ROWS, conv_rows, 0)

    x = x_ref[0]
    h = _rmsnorm_rows(x, gmix_ref[0]).astype(BF16)
    conv = _dot(cn_ref[...], wco_ref[0])
    sga = _sigmoid(_dot(h, wg_ref[0, :, 0:D_MODEL]))
    sgc = _sigmoid(_dot(h, wg_ref[0, :, D_MODEL:2 * D_MODEL]))
    merged = sga * attn_ref[0].astype(F32) + sgc * conv
    o_ref[0] = x + _dot(merged.astype(BF16), wo_ref[0])


def _mix(x, c, attn, layer, gmix, wg, wdw8, dwb, lng, lnb, wco, wo):
    B, T, _ = x.shape
    tm = min(MIX_TM, T)
    nt = T // tm
    hb = tm // HALO
    n_halo = T // HALO
    tile = lambda b, i: (b, i, 0)
    lay = lambda b, i: (layer, 0, 0)
    in_specs = [
        pl.BlockSpec((1, tm, D_MODEL), tile),
        pl.BlockSpec((1, HALO, CONV_DIM), lambda b, i: (b, jnp.maximum(i * hb - 1, 0), 0)),
        pl.BlockSpec((1, tm, CONV_DIM), tile),
        pl.BlockSpec((1, HALO, CONV_DIM), lambda b, i: (b, jnp.minimum((i + 1) * hb, n_halo - 1), 0)),
        pl.BlockSpec((1, tm, D_MODEL), tile),
        _const_spec((1, 1, D_MODEL), lay),
        _const_spec((1, D_MODEL, 2 * D_MODEL), lay),
        _const_spec((1, CONV_K, SUBLANES, CONV_DIM), lambda b, i: (layer, 0, 0, 0)),
        _const_spec((1, 1, CONV_DIM), lay),
        _const_spec((1, 1, CONV_DIM), lay),
        _const_spec((1, 1, CONV_DIM), lay),
        _const_spec((1, CONV_DIM, D_MODEL), lay),
        _const_spec((1, D_MODEL, D_MODEL), lay),
    ]
    return pl.pallas_call(
        functools.partial(_mix_kernel, n_tiles=nt),
        grid=(B, nt), in_specs=in_specs,
        out_specs=pl.BlockSpec((1, tm, D_MODEL), tile),
        out_shape=jax.ShapeDtypeStruct((B, T, D_MODEL), F32),
        scratch_shapes=[pltpu.VMEM((SUBLANES, tm + 2 * HALO, CONV_DIM), F32),
                        pltpu.VMEM((tm, CONV_DIM), BF16)],
        compiler_params=pltpu.CompilerParams(
            dimension_semantics=("parallel", "parallel"), vmem_limit_bytes=VMEM_LIMIT),
        name="mix",
    )(x, c, c, c, attn, gmix, wg, wdw8, dwb, lng, lnb, wco, wo)


def _mlp_kernel(x_ref, g_ref, wup_ref, wdn_ref, gfin_ref, o_ref, *, final):
    x = x_ref[...]
    h = _rmsnorm_rows(x, g_ref[0]).astype(BF16)
    y = x
    for j in range(D_FF // MLP_FC):
        cols = slice(j * MLP_FC, (j + 1) * MLP_FC)
        up = _dot(h, wup_ref[0, :, cols])
        act = jnp.square(jnp.maximum(up, 0.0)).astype(BF16)
        y = y + _dot(act, wdn_ref[0, cols, :])
    if final:
        y = _rmsnorm_rows(y, gfin_ref[...])
    o_ref[...] = y


def _mlp(x, layer, g, wup, wdn, gfin, final):
    B, T, _ = x.shape
    n = B * T
    tm = min(MLP_TM, n)
    x2 = x.reshape(n, D_MODEL)
    lay = lambda i: (layer, 0, 0)
    out = pl.pallas_call(
        functools.partial(_mlp_kernel, final=final),
        grid=(n // tm,),
        in_specs=[
            pl.BlockSpec((tm, D_MODEL), lambda i: (i, 0)),
            _const_spec((1, 1, D_MODEL), lay),
            _const_spec((1, D_MODEL, D_FF), lay),
            _const_spec((1, D_FF, D_MODEL), lay),
            _const_spec((1, D_MODEL), lambda i: (0, 0)),
        ],
        out_specs=pl.BlockSpec((tm, D_MODEL), lambda i: (i, 0)),
        out_shape=jax.ShapeDtypeStruct((n, D_MODEL), F32),
        compiler_params=pltpu.CompilerParams(
            dimension_semantics=("parallel",), vmem_limit_bytes=VMEM_LIMIT),
        name="mlp",
    )(x2, g, wup, wdn, gfin)
    return out.reshape(B, T, D_MODEL)


def _rope_table_t(T):
    rows = T // GRID_W
    row = jnp.repeat(jnp.arange(rows, dtype=F32), GRID_W)
    col = jnp.tile(jnp.arange(GRID_W, dtype=F32), rows)
    freqs = ROPE_THETA ** (-jnp.arange(0, AXIS_DIM, 2, dtype=F32) / AXIS_DIM)
    ang_r = row[:, None] * freqs[None, :]
    ang_c = col[:, None] * freqs[None, :]
    tab = jnp.concatenate([jnp.cos(ang_r), jnp.sin(ang_r), jnp.cos(ang_c), jnp.sin(ang_c)], axis=1)
    return tab.T


def _trunk(x, p):
    T = x.shape[1]
    rope_t = _rope_table_t(T)
    for l in range(DEPTH):
        q_t, k, v_t, c = _proj(x, l, p["gmix"], p["wqkv_t"], p["wu"], p["qg"], p["kg"], rope_t)
        attn = _attention(q_t, k, v_t)
        x = _mix(x, c, attn, l, p["gmix"], p["wg"], p["wdw8"], p["dwb"], p["lng"], p["lnb"],
                 p["wco"], p["wo"])
        x = _mlp(x, l, p["gmlp"], p["wup"], p["wdn"], p["gfin"], final=(l == DEPTH - 1))
    return x


def kernel(x_prompt, x_sample, g_mix, w_in, q_norm, k_norm, conv_dw, conv_dw_b, conv_ln_g,
           conv_ln_b, w_conv_out, w_o, g_mlp, w_up, w_down, g_final):
    u0 = QKV_DIM
    g0 = QKV_DIM + 2 * CONV_DIM
    p = {
        "gmix": g_mix.reshape(DEPTH, 1, D_MODEL),
        "wqkv_t": jnp.swapaxes(w_in[:, :, :u0], 1, 2).astype(BF16),
        "wu": w_in[:, :, u0:g0].astype(BF16),
        "wg": w_in[:, :, g0:].astype(BF16),
        "qg": q_norm.reshape(DEPTH, HEAD_DIM, 1),
        "kg": k_norm.reshape(DEPTH, HEAD_DIM, 1),
        "wdw8": jnp.broadcast_to(conv_dw[:, :, None, :], (DEPTH, CONV_K, SUBLANES, CONV_DIM)),
        "dwb": conv_dw_b.reshape(DEPTH, 1, CONV_DIM),
        "lng": conv_ln_g.reshape(DEPTH, 1, CONV_DIM),
        "lnb": conv_ln_b.reshape(DEPTH, 1, CONV_DIM),
        "wco": w_conv_out.astype(BF16),
        "wo": w_o.astype(BF16),
        "gmlp": g_mlp.reshape(DEPTH, 1, D_MODEL),
        "wup": w_up.astype(BF16),
        "wdn": w_down.astype(BF16),
        "gfin": g_final.reshape(1, D_MODEL),
    }
    return (_trunk(x_prompt, p), _trunk(x_sample, p))
```

```python
import functools
import math

import jax
import jax.numpy as jnp
from jax import lax
from jax.experimental import pallas as pl
from jax.experimental.pallas import tpu as pltpu

D_MODEL = 1024
DEPTH = 4
HEAD_DIM = 64
N_Q_HEADS = 16
N_KV_HEADS = 4
GROUP = N_Q_HEADS // N_KV_HEADS
Q_DIM = N_Q_HEADS * HEAD_DIM
KV_DIM = N_KV_HEADS * HEAD_DIM
QKV_DIM = Q_DIM + 2 * KV_DIM
ROPE_THETA = 10000.0
AXIS_DIM = HEAD_DIM // 2
HALF_AXIS = AXIS_DIM // 2
GRID_W = 64
CONV_DIM = D_MODEL
CONV_K = 31
CONV_PAD = (CONV_K - 1) // 2
D_FF = 4 * D_MODEL
EPS = 1e-6

SUBLANES = 8
LANES = 128
BF16_SUBLANES = 16
HALO = BF16_SUBLANES
ACC_ROWS = HEAD_DIM + BF16_SUBLANES
VMEM_LIMIT = 56 * 1024 * 1024

Q_SCALE = math.log2(math.e) / math.sqrt(HEAD_DIM)

BF16 = jnp.bfloat16
F32 = jnp.float32

PROJ_TM = 512
ATTN_TQ = 512
ATTN_QT = 2
MIX_TM = 256
MIX_ROWS = 64
MIX_LANES = LANES
MLP_TM = 512
MLP_FC = 1024


def _dot(a, b):
    return jnp.dot(a, b, preferred_element_type=F32)


def _dot_nt(a, b):
    return lax.dot_general(a, b, (((1,), (1,)), ((), ())), preferred_element_type=F32)


def _sigmoid(x):
    return 1.0 / (1.0 + jnp.exp(-x))


def _rmsnorm_rows(x, g):
    ms = jnp.mean(x * x, axis=-1, keepdims=True)
    return x * lax.rsqrt(ms + EPS) * g


def _const_spec(shape, index_map):
    return pl.BlockSpec(shape, index_map, pipeline_mode=pl.Buffered(1))


def _proj_kernel(x_ref, gmix_ref, wqkv_ref, wu_ref, qg_ref, kg_ref, rope_ref,
                 qt_ref, k_ref, vt_ref, c_ref):
    x = x_ref[0]
    h = _rmsnorm_rows(x, gmix_ref[0]).astype(BF16)

    qkv_t = _dot_nt(wqkv_ref[0], h)
    rope = rope_ref[...]
    cr, sr = rope[0:HALF_AXIS], rope[HALF_AXIS:2 * HALF_AXIS]
    cc, sc = rope[2 * HALF_AXIS:3 * HALF_AXIS], rope[3 * HALF_AXIS:4 * HALF_AXIS]

    def norm_rope(a, g):
        r = lax.rsqrt(jnp.mean(a * a, axis=0, keepdims=True) + EPS)
        an = a * r * g
        x1r, x2r = an[0:HALF_AXIS], an[HALF_AXIS:2 * HALF_AXIS]
        x1c, x2c = an[2 * HALF_AXIS:3 * HALF_AXIS], an[3 * HALF_AXIS:4 * HALF_AXIS]
        return jnp.concatenate(
            [x1r * cr - x2r * sr, x2r * cr + x1r * sr,
             x1c * cc - x2c * sc, x2c * cc + x1c * sc], axis=0)

    qg = qg_ref[0] * Q_SCALE
    kg = kg_ref[0]
    for hh in range(N_Q_HEADS):
        a = qkv_t[hh * HEAD_DIM:(hh + 1) * HEAD_DIM]
        qt_ref[0, hh * HEAD_DIM:(hh + 1) * HEAD_DIM, :] = norm_rope(a, qg).astype(BF16)
    for hh in range(N_KV_HEADS):
        a = qkv_t[Q_DIM + hh * HEAD_DIM:Q_DIM + (hh + 1) * HEAD_DIM]
        k_ref[0, hh] = norm_rope(a, kg).T.astype(BF16)
    vt_ref[0, 0] = qkv_t[Q_DIM + KV_DIM:].astype(BF16)

    half = CONV_DIM // 2
    for j in range(2):
        ua = _dot(h, wu_ref[0, :, j * half:(j + 1) * half])
        ub = _dot(h, wu_ref[0, :, CONV_DIM + j * half:CONV_DIM + (j + 1) * half])
        c_ref[0, :, j * half:(j + 1) * half] = (ua * _sigmoid(ub)).astype(BF16)


def _proj(x, layer, gmix, wqkv_t, wu, qg, kg, rope_t):
    B, T, _ = x.shape
    tm = min(PROJ_TM, T)
    nt = T // tm
    grid = (B, nt)
    out_shape = (
        jax.ShapeDtypeStruct((B, Q_DIM, T), BF16),
        jax.ShapeDtypeStruct((B, N_KV_HEADS, T, HEAD_DIM), BF16),
        jax.ShapeDtypeStruct((B, nt, KV_DIM, tm), BF16),
        jax.ShapeDtypeStruct((B, T, CONV_DIM), BF16),
    )
    in_specs = [
        pl.BlockSpec((1, tm, D_MODEL), lambda b, i: (b, i, 0)),
        _const_spec((1, 1, D_MODEL), lambda b, i: (layer, 0, 0)),
        _const_spec((1, QKV_DIM, D_MODEL), lambda b, i: (layer, 0, 0)),
        _const_spec((1, D_MODEL, 2 * CONV_DIM), lambda b, i: (layer, 0, 0)),
        _const_spec((1, HEAD_DIM, 1), lambda b, i: (layer, 0, 0)),
        _const_spec((1, HEAD_DIM, 1), lambda b, i: (layer, 0, 0)),
        pl.BlockSpec((4 * HALF_AXIS, tm), lambda b, i: (0, i)),
    ]
    out_specs = (
        pl.BlockSpec((1, Q_DIM, tm), lambda b, i: (b, 0, i)),
        pl.BlockSpec((1, N_KV_HEADS, tm, HEAD_DIM), lambda b, i: (b, 0, i, 0)),
        pl.BlockSpec((1, 1, KV_DIM, tm), lambda b, i: (b, i, 0, 0)),
        pl.BlockSpec((1, tm, CONV_DIM), lambda b, i: (b, i, 0)),
    )
    return pl.pallas_call(
        _proj_kernel, grid=grid, in_specs=in_specs, out_specs=out_specs, out_shape=out_shape,
        compiler_params=pltpu.CompilerParams(
            dimension_semantics=("parallel", "parallel"), vmem_limit_bytes=VMEM_LIMIT),
        name="proj",
    )(x, gmix, wqkv_t, wu, qg, kg, rope_t)


def _attn_kernel(qt_ref, k_ref, vt_ref, o_ref, acc_ref, m_ref, s_ref, mc_ref, p_ref, al_ref,
                 *, n_kv_blocks, ts, tq):
    n_q = qt_ref.shape[2] // tq
    acc_ref[...] = jnp.zeros_like(acc_ref)
    m_ref[...] = jnp.full_like(m_ref, -jnp.inf)
    hs = ts // 2
    n_steps = 2 * n_kv_blocks
    ones = jnp.ones((ACC_ROWS - HEAD_DIM, hs), BF16)

    def k_half(si, h):
        return k_ref[0, 0, pl.ds(pl.multiple_of(si * ts + h * hs, hs), hs), :]

    def v_half(si, h):
        return jnp.concatenate([vt_ref[0, si, :, h * hs:(h + 1) * hs], ones], axis=0)

    def scores(qi, kb, g, slot):
        q = qt_ref[0, g * HEAD_DIM:(g + 1) * HEAD_DIM, qi * tq:(qi + 1) * tq]
        s = _dot(kb, q)
        s_ref[qi, slot, g] = s
        mc_ref[qi, slot, g] = jnp.max(s, axis=0, keepdims=True)

    def softmax(qi, g, slot):
        m_prev = m_ref[qi, g]
        m_new = jnp.maximum(m_prev, mc_ref[qi, slot, g])
        al_ref[qi, slot, g] = jnp.exp2(m_prev - m_new)
        p_ref[qi, slot, g] = jnp.exp2(s_ref[qi, slot, g] - m_new).astype(BF16)
        m_ref[qi, g] = m_new

    def pv(qi, vb, g, slot):
        acc_ref[qi, g] = al_ref[qi, slot, g] * acc_ref[qi, g] + _dot(vb, p_ref[qi, slot, g])

    def step(kb, sc_slot, sm_slot, vb, pv_slot):
        for g in range(GROUP):
            for qi in range(n_q):
                if sm_slot is not None:
                    softmax(qi, g, sm_slot)
                if kb is not None:
                    scores(qi, kb, g, sc_slot)
                if vb is not None:
                    pv(qi, vb, g, pv_slot)

    step(k_half(0, 0), 0, None, None, None)
    step(k_half(0, 1), 1, None, None, None)
    step(k_half(1, 0), 2, 0, None, None)
    step(k_half(1, 1), 3, 1, None, None)

    def two_steps(j, e):
        step(k_half(j, 0), e, 2 - e, v_half(j - 2, 0), e)
        step(k_half(j, 1), e + 1, 3 - e, v_half(j - 2, 1), e + 1)

    def body(j, carry):
        lax.cond(lax.rem(j, 2) == 0, lambda: two_steps(j, 0), lambda: two_steps(j, 2))
        return carry

    lax.fori_loop(2, n_kv_blocks, body, 0)
    for t in range(n_steps, n_steps + 4):
        sm = (t - 2) % 4 if t - 2 < n_steps else None
        u = t - 4
        step(None, None, sm, v_half(u // 2, u % 2), u % 4)
    for qi in range(n_q):
        outs = []
        for g in range(GROUP):
            a = acc_ref[qi, g]
            outs.append(a[0:HEAD_DIM] * (1.0 / a[HEAD_DIM:HEAD_DIM + 1]))
        o_ref[0, qi * tq:(qi + 1) * tq, :] = jnp.concatenate(outs, axis=0).T.astype(BF16)


def _attention(q_t, k, v_t):
    B, _, T = q_t.shape
    n_s, ts = v_t.shape[1], v_t.shape[3]
    tq = min(ATTN_TQ, T)
    nq = min(ATTN_QT, T // tq)
    grid = (B, N_KV_HEADS, T // (tq * nq))
    gw = GROUP * HEAD_DIM
    assert n_s >= 2, "the pipeline fill covers two kv blocks"
    slots = 4
    stat = pltpu.VMEM((nq, slots, GROUP, 1, tq), F32)
    return pl.pallas_call(
        functools.partial(_attn_kernel, n_kv_blocks=n_s, ts=ts, tq=tq),
        grid=grid,
        in_specs=[
            pl.BlockSpec((1, gw, nq * tq), lambda b, h, i: (b, h, i)),
            pl.BlockSpec((1, 1, T, HEAD_DIM), lambda b, h, i: (b, h, 0, 0)),
            pl.BlockSpec((1, n_s, HEAD_DIM, ts), lambda b, h, i: (b, 0, h, 0)),
        ],
        out_specs=pl.BlockSpec((1, nq * tq, gw), lambda b, h, i: (b, i, h)),
        out_shape=jax.ShapeDtypeStruct((B, T, Q_DIM), BF16),
        scratch_shapes=[pltpu.VMEM((nq, GROUP, ACC_ROWS, tq), F32),
                        pltpu.VMEM((nq, GROUP, 1, tq), F32),
                        pltpu.VMEM((nq, slots, GROUP, ts // 2, tq), F32), stat,
                        pltpu.VMEM((nq, slots, GROUP, ts // 2, tq), BF16), stat],
        compiler_params=pltpu.CompilerParams(
            dimension_semantics=("parallel", "parallel", "parallel"), vmem_limit_bytes=VMEM_LIMIT),
        name="attn",
    )(q_t, k, v_t)


def _mix_kernel(x_ref, cprev_ref, c_ref, cnext_ref, attn_ref, gmix_ref, wg_ref, wdw_ref, dwb_ref,
                lng_ref, lnb_ref, wco_ref, wo_ref, o_ref, sh_ref, cn_ref, *, n_tiles):
    i = pl.program_id(1)
    tm = c_ref.shape[1]
    span = tm + 3 * SUBLANES

    prev = cprev_ref[0].astype(F32)
    nxt = cnext_ref[0].astype(F32)
    sh_ref[0, 0:HALO, :] = jnp.where(i > 0, prev, 0.0)
    sh_ref[0, HALO:HALO + tm, :] = c_ref[0].astype(F32)
    sh_ref[0, HALO + tm:2 * HALO + tm, :] = jnp.where(i < n_tiles - 1, nxt, 0.0)
    for b in range(1, SUBLANES):
        sh_ref[b, 0:span, :] = sh_ref[0, b:b + span, :]

    dwb = dwb_ref[0]
    lng = lng_ref[0]
    lnb = lnb_ref[0]
    reps = MIX_ROWS // SUBLANES
    first = HALO - CONV_PAD

    def conv_rows(rb, carry):
        r0 = pl.multiple_of(rb * MIX_ROWS, MIX_ROWS)
        chunks = []
        for lc in range(CONV_DIM // MIX_LANES):
            lanes = slice(lc * MIX_LANES, (lc + 1) * MIX_LANES)
            accs = [jnp.zeros((SUBLANES, MIX_LANES), F32) for _ in range(reps)]
            for b in range(SUBLANES):
                taps = [off - first for off in range(b, first + CONV_K, SUBLANES)
                        if 0 <= off - first < CONV_K]
                groups = {}
                for kk in taps:
                    a = (first + kk) // SUBLANES
                    w = wdw_ref[0, kk, :, lanes]
                    for r in range(reps):
                        if a + r not in groups:
                            rows = pl.ds(r0 + (a + r) * SUBLANES, SUBLANES)
                            groups[a + r] = sh_ref[b, rows, lanes]
                        accs[r] = accs[r] + groups[a + r] * w
            chunks.append(jnp.concatenate(accs, axis=0))
        y = jnp.concatenate(chunks, axis=1) + dwb
        mu = jnp.mean(y, axis=-1, keepdims=True)
        yc = y - mu
        var = jnp.mean(yc * yc, axis=-1, keepdims=True)
        z = yc * lax.rsqrt(var + EPS) * lng + lnb
        cn_ref[pl.ds(r0, MIX_ROWS), :] = (z * _sigmoid(z)).astype(BF16)
        return carry

    lax.fori_loop(0, tm // MIX_ROWS, conv_rows, 0)

    x = x_ref[0]
    h = _rmsnorm_rows(x, gmix_ref[0]).astype(BF16)
    conv = _dot(cn_ref[...], wco_ref[0])
    sga = _sigmoid(_dot(h, wg_ref[0, :, 0:D_MODEL]))
    sgc = _sigmoid(_dot(h, wg_ref[0, :, D_MODEL:2 * D_MODEL]))
    merged = sga * attn_ref[0].astype(F32) + sgc * conv
    o_ref[0] = x + _dot(merged.astype(BF16), wo_ref[0])


def _mix(x, c, attn, layer, gmix, wg, wdw8, dwb, lng, lnb, wco, wo):
    B, T, _ = x.shape
    tm = min(MIX_TM, T)
    nt = T // tm
    hb = tm // HALO
    n_halo = T // HALO
    tile = lambda b, i: (b, i, 0)
    lay = lambda b, i: (layer, 0, 0)
    in_specs = [
        pl.BlockSpec((1, tm, D_MODEL), tile),
        pl.BlockSpec((1, HALO, CONV_DIM), lambda b, i: (b, jnp.maximum(i * hb - 1, 0), 0)),
        pl.BlockSpec((1, tm, CONV_DIM), tile),
        pl.BlockSpec((1, HALO, CONV_DIM), lambda b, i: (b, jnp.minimum((i + 1) * hb, n_halo - 1), 0)),
        pl.BlockSpec((1, tm, D_MODEL), tile),
        _const_spec((1, 1, D_MODEL), lay),
        _const_spec((1, D_MODEL, 2 * D_MODEL), lay),
        _const_spec((1, CONV_K, SUBLANES, CONV_DIM), lambda b, i: (layer, 0, 0, 0)),
        _const_spec((1, 1, CONV_DIM), lay),
        _const_spec((1, 1, CONV_DIM), lay),
        _const_spec((1, 1, CONV_DIM), lay),
        _const_spec((1, CONV_DIM, D_MODEL), lay),
        _const_spec((1, D_MODEL, D_MODEL), lay),
    ]
    return pl.pallas_call(
        functools.partial(_mix_kernel, n_tiles=nt),
        grid=(B, nt), in_specs=in_specs,
        out_specs=pl.BlockSpec((1, tm, D_MODEL), tile),
        out_shape=jax.ShapeDtypeStruct((B, T, D_MODEL), F32),
        scratch_shapes=[pltpu.VMEM((SUBLANES, tm + 2 * HALO, CONV_DIM), F32),
                        pltpu.VMEM((tm, CONV_DIM), BF16)],
        compiler_params=pltpu.CompilerParams(
            dimension_semantics=("parallel", "parallel"), vmem_limit_bytes=VMEM_LIMIT),
        name="mix",
    )(x, c, c, c, attn, gmix, wg, wdw8, dwb, lng, lnb, wco, wo)


def _mlp_kernel(x_ref, g_ref, wup_ref, wdn_ref, gfin_ref, o_ref, *, final):
    x = x_ref[...]
    h = _rmsnorm_rows(x, g_ref[0]).astype(BF16)
    y = x
    for j in range(D_FF // MLP_FC):
        cols = slice(j * MLP_FC, (j + 1) * MLP_FC)
        up = _dot(h, wup_ref[0, :, cols])
        act = jnp.square(jnp.maximum(up, 0.0)).astype(BF16)
        y = y + _dot(act, wdn_ref[0, cols, :])
    if final:
        y = _rmsnorm_rows(y, gfin_ref[...])
    o_ref[...] = y


def _mlp(x, layer, g, wup, wdn, gfin, final):
    B, T, _ = x.shape
    n = B * T
    tm = min(MLP_TM, n)
    x2 = x.reshape(n, D_MODEL)
    lay = lambda i: (layer, 0, 0)
    out = pl.pallas_call(
        functools.partial(_mlp_kernel, final=final),
        grid=(n // tm,),
        in_specs=[
            pl.BlockSpec((tm, D_MODEL), lambda i: (i, 0)),
            _const_spec((1, 1, D_MODEL), lay),
            _const_spec((1, D_MODEL, D_FF), lay),
            _const_spec((1, D_FF, D_MODEL), lay),
            _const_spec((1, D_MODEL), lambda i: (0, 0)),
        ],
        out_specs=pl.BlockSpec((tm, D_MODEL), lambda i: (i, 0)),
        out_shape=jax.ShapeDtypeStruct((n, D_MODEL), F32),
        compiler_params=pltpu.CompilerParams(
            dimension_semantics=("parallel",), vmem_limit_bytes=VMEM_LIMIT),
        name="mlp",
    )(x2, g, wup, wdn, gfin)
    return out.reshape(B, T, D_MODEL)


def _rope_table_t(T):
    rows = T // GRID_W
    row = jnp.repeat(jnp.arange(rows, dtype=F32), GRID_W)
    col = jnp.tile(jnp.arange(GRID_W, dtype=F32), rows)
    freqs = ROPE_THETA ** (-jnp.arange(0, AXIS_DIM, 2, dtype=F32) / AXIS_DIM)
    ang_r = row[:, None] * freqs[None, :]
    ang_c = col[:, None] * freqs[None, :]
    tab = jnp.concatenate([jnp.cos(ang_r), jnp.sin(ang_r), jnp.cos(ang_c), jnp.sin(ang_c)], axis=1)
    return tab.T


def _trunk(x, p):
    T = x.shape[1]
    rope_t = _rope_table_t(T)
    for l in range(DEPTH):
        q_t, k, v_t, c = _proj(x, l, p["gmix"], p["wqkv_t"], p["wu"], p["qg"], p["kg"], rope_t)
        attn = _attention(q_t, k, v_t)
        x = _mix(x, c, attn, l, p["gmix"], p["wg"], p["wdw8"], p["dwb"], p["lng"], p["lnb"],
                 p["wco"], p["wo"])
        x = _mlp(x, l, p["gmlp"], p["wup"], p["wdn"], p["gfin"], final=(l == DEPTH - 1))
    return x


def kernel(x_prompt, x_sample, g_mix, w_in, q_norm, k_norm, conv_dw, conv_dw_b, conv_ln_g,
           conv_ln_b, w_conv_out, w_o, g_mlp, w_up, w_down, g_final):
    u0 = QKV_DIM
    g0 = QKV_DIM + 2 * CONV_DIM
    p = {
        "gmix": g_mix.reshape(DEPTH, 1, D_MODEL),
        "wqkv_t": jnp.swapaxes(w_in[:, :, :u0], 1, 2).astype(BF16),
        "wu": w_in[:, :, u0:g0].astype(BF16),
        "wg": w_in[:, :, g0:].astype(BF16),
        "qg": q_norm.reshape(DEPTH, HEAD_DIM, 1),
        "kg": k_norm.reshape(DEPTH, HEAD_DIM, 1),
        "wdw8": jnp.broadcast_to(conv_dw[:, :, None, :], (DEPTH, CONV_K, SUBLANES, CONV_DIM)),
        "dwb": conv_dw_b.reshape(DEPTH, 1, CONV_DIM),
        "lng": conv_ln_g.reshape(DEPTH, 1, CONV_DIM),
        "lnb": conv_ln_b.reshape(DEPTH, 1, CONV_DIM),
        "wco": w_conv_out.astype(BF16),
        "wo": w_o.astype(BF16),
        "gmlp": g_mlp.reshape(DEPTH, 1, D_MODEL),
        "wup": w_up.astype(BF16),
        "wdn": w_down.astype(BF16),
        "gfin": g_final.reshape(1, D_MODEL),
    }
    return (_trunk(x_prompt, p), _trunk(x_sample, p))
```
